```python
import math
import jax, jax.numpy as jnp
from jax import lax
import numpy as np

D_MODEL = 1024
BATCH = 2
SEQ = 8192
DEPTH = 1

N_HEADS = 8
HEAD_DIM = 64
ATT_WIDTH = N_HEADS * HEAD_DIM
LRU_WIDTH = 1024
LRU_BLOCKS = 8
LRU_BLOCK_DIM = LRU_WIDTH // LRU_BLOCKS
LRU_C = 8.0
CONV_WIDTH = 4
IN_PROJ = 3 * ATT_WIDTH + 2 * LRU_WIDTH
PL_DIM = 256
N_EXPERTS = 32
TOP_K = 4
D_EXPERT = 1024
SWIGLU_LIMIT = 7.0
SWIGLU_ALPHA = 1.702
Q_BLOCK = 128
MOE_BLOCK = 128
EPS = 1e-6

kernel_name = "hybrid_stickbreak_rglru_moe_block"


def rmsnorm(x, g):
    x32 = x.astype(jnp.float32)
    y = x32 * lax.rsqrt(jnp.mean(x32 * x32, axis=-1, keepdims=True) + EPS) * g.astype(jnp.float32)
    return y.astype(x.dtype)


def stick_breaking_attention(q, k, v):
    B, S, _ = q.shape
    n_qb = S // Q_BLOCK
    qh = q.reshape(B, S, N_HEADS, HEAD_DIM).astype(jnp.float32) * (1.0 / math.sqrt(HEAD_DIM))
    kh = k.reshape(B, S, N_HEADS, HEAD_DIM).astype(jnp.float32)
    vh = v.reshape(B, S, N_HEADS, HEAD_DIM).astype(jnp.float32)
    qb = qh.reshape(B, n_qb, Q_BLOCK, N_HEADS, HEAD_DIM).transpose(1, 0, 3, 2, 4)
    kpos = jnp.arange(S)

    def block(args):
        q_blk, blk = args
        qpos = blk * Q_BLOCK + jnp.arange(Q_BLOCK)
        z = jnp.einsum('bhqd,bkhd->bhqk', q_blk, kh)
        mask = kpos[None, :] < qpos[:, None]
        log_keep = jnp.where(mask, jax.nn.log_sigmoid(-z), 0.0)
        log_between = lax.cumsum(log_keep, axis=3, reverse=True) - log_keep
        w = jnp.where(mask, jnp.exp(jax.nn.log_sigmoid(z) + log_between), 0.0)
        return jnp.einsum('bhqk,bkhd->bqhd', w, vh)

    out = lax.map(block, (qb, jnp.arange(n_qb)))
    return out.transpose(1, 0, 2, 3, 4).reshape(B, S, ATT_WIDTH).astype(q.dtype)


def causal_depthwise_conv(u, w, b):
    S = u.shape[1]
    up = jnp.pad(u, ((0, 0), (CONV_WIDTH - 1, 0), (0, 0)))
    out = b
    for j in range(CONV_WIDTH):
        out = out + up[:, j:j + S] * w[j]
    return out


def rglru(u, wa, ba, wx, bx, lam):
    B, S, W = u.shape
    ub = u.reshape(B, S, LRU_BLOCKS, LRU_BLOCK_DIM)
    r = jax.nn.sigmoid(jnp.einsum('bsnc,ncd->bsnd', ub, wa).reshape(B, S, W) + ba)
    i = jax.nn.sigmoid(jnp.einsum('bsnc,ncd->bsnd', ub, wx).reshape(B, S, W) + bx)
    log_a = LRU_C * r.astype(jnp.float32) * jax.nn.log_sigmoid(lam.astype(jnp.float32))
    a = jnp.exp(log_a)
    b_in = jnp.sqrt(-jnp.expm1(2.0 * log_a)) * (i * u).astype(jnp.float32)

    def step(h, ab):
        a_t, b_t = ab
        h = a_t * h + b_t
        return h, h

    _, hs = lax.scan(step, jnp.zeros((B, W), jnp.float32),
                     (a.transpose(1, 0, 2), b_in.transpose(1, 0, 2)))
    return hs.transpose(1, 0, 2).astype(u.dtype)


def token_mixing(xn, w_in, conv_w, conv_b, lru_wa, lru_ba, lru_wx, lru_bx, lru_lambda,
                 w_att_out, w_lru_out, w_merge, b_merge, w_out):
    proj = xn @ w_in
    q, k, v, u, g_in = jnp.split(
        proj, [ATT_WIDTH, 2 * ATT_WIDTH, 3 * ATT_WIDTH, 3 * ATT_WIDTH + LRU_WIDTH], axis=-1)
    y_att = stick_breaking_attention(q, k, v) @ w_att_out
    h_lru = rglru(causal_depthwise_conv(u, conv_w, conv_b), lru_wa, lru_ba, lru_wx, lru_bx, lru_lambda)
    y_lru = (jax.nn.gelu(g_in) * h_lru) @ w_lru_out
    gates = jax.nn.sigmoid(xn @ w_merge + b_merge)
    g_att, g_lru = jnp.split(gates, 2, axis=-1)
    return (g_att * y_att + g_lru * y_lru) @ w_out


def clamped_swiglu_expert(xb, w_gu, b_gu, w_dn, b_dn):
    hgu = xb @ w_gu + b_gu
    gate, up = jnp.split(hgu, 2, axis=-1)
    gate = jnp.minimum(gate, SWIGLU_LIMIT)
    up = jnp.clip(up, -SWIGLU_LIMIT, SWIGLU_LIMIT)
    glu = gate * jax.nn.sigmoid(gate * SWIGLU_ALPHA)
    return ((up + 1.0) * glu) @ w_dn + b_dn


def moe(xn, w_router, b_router, w_gate_up, b_gate_up, w_down, b_down):
    B, S, D = xn.shape
    n_tok = B * S
    xf = xn.reshape(n_tok, D)
    logits = (xf @ w_router + b_router).astype(jnp.float32)
    top_v, top_e = lax.top_k(logits, TOP_K)
    gates = jax.nn.softmax(top_v, axis=-1)
    n_asg = n_tok * TOP_K
    e_flat = top_e.reshape(n_asg)
    tok_flat = jnp.arange(n_asg, dtype=jnp.int32) // TOP_K
    g_flat = gates.reshape(n_asg)
    order = jnp.argsort(e_flat)
    e_sorted, tok_sorted, g_sorted = e_flat[order], tok_flat[order], g_flat[order]
    counts = jnp.bincount(e_flat, length=N_EXPERTS)
    padded = (counts + MOE_BLOCK - 1) // MOE_BLOCK * MOE_BLOCK
    start = jnp.cumsum(counts) - counts
    pad_end = jnp.cumsum(padded)
    pad_start = pad_end - padded
    dest = pad_start[e_sorted] + jnp.arange(n_asg) - start[e_sorted]
    n_rows = (n_asg + MOE_BLOCK - 1) // MOE_BLOCK * MOE_BLOCK + N_EXPERTS * MOE_BLOCK
    n_blocks = n_rows // MOE_BLOCK
    row_tok = jnp.zeros((n_rows,), jnp.int32).at[dest].set(tok_sorted)
    row_gate = jnp.zeros((n_rows,), jnp.float32).at[dest].set(g_sorted)
    block_expert = jnp.minimum(
        jnp.searchsorted(pad_end, jnp.arange(n_blocks) * MOE_BLOCK, side='right'), N_EXPERTS - 1)
    xb = xf[row_tok].reshape(n_blocks, MOE_BLOCK, D)

    def expert_block(args):
        x_blk, e = args
        return clamped_swiglu_expert(x_blk, w_gate_up[e], b_gate_up[e], w_down[e], b_down[e])

    yb = lax.map(expert_block, (xb, block_expert))
    y_rows = yb.reshape(n_rows, D).astype(jnp.float32) * row_gate[:, None]
    y = jax.ops.segment_sum(y_rows, row_tok, num_segments=n_tok)
    return y.reshape(B, S, D).astype(xn.dtype)


def setup_inputs(seed: int = 0) -> dict:
    key = jax.random.key(seed)
    ks = jax.random.split(key, 32)

    def nrm(k, shape, scale):
        return jax.random.normal(k, shape, jnp.float32) * scale

    u = jax.random.uniform(ks[10], (DEPTH, LRU_WIDTH), jnp.float32, minval=0.9, maxval=0.999)
    a0 = u ** (1.0 / LRU_C)
    lru_lambda = jnp.log(a0) - jnp.log1p(-a0)
    return {
        "x": nrm(ks[0], (BATCH, SEQ, D_MODEL), 1.0),
        "p": nrm(ks[1], (DEPTH, BATCH, SEQ, PL_DIM), 1.0),
        "norm_mix_g": 1.0 + nrm(ks[2], (DEPTH, D_MODEL), 0.02),
        "w_in": nrm(ks[3], (DEPTH, D_MODEL, IN_PROJ), D_MODEL ** -0.5),
        "conv_w": nrm(ks[4], (DEPTH, CONV_WIDTH, LRU_WIDTH), CONV_WIDTH ** -0.5),
        "conv_b": nrm(ks[5], (DEPTH, LRU_WIDTH), 0.01),
        "lru_wa": nrm(ks[6], (DEPTH, LRU_BLOCKS, LRU_BLOCK_DIM, LRU_BLOCK_DIM), LRU_BLOCK_DIM ** -0.5),
        "lru_ba": nrm(ks[7], (DEPTH, LRU_WIDTH), 0.01),
        "lru_wx": nrm(ks[8], (DEPTH, LRU_BLOCKS, LRU_BLOCK_DIM, LRU_BLOCK_DIM), LRU_BLOCK_DIM ** -0.5),
        "lru_bx": nrm(ks[9], (DEPTH, LRU_WIDTH), 0.01),
        "lru_lambda": lru_lambda,
        "w_att_out": nrm(ks[11], (DEPTH, ATT_WIDTH, D_MODEL), ATT_WIDTH ** -0.5),
        "w_lru_out": nrm(ks[12], (DEPTH, LRU_WIDTH, D_MODEL), LRU_WIDTH ** -0.5),
        "w_merge": nrm(ks[13], (DEPTH, D_MODEL, 2 * D_MODEL), D_MODEL ** -0.5),
        "b_merge": nrm(ks[14], (DEPTH, 2 * D_MODEL), 0.01),
        "w_out": nrm(ks[15], (DEPTH, D_MODEL, D_MODEL), D_MODEL ** -0.5),
        "norm_moe_g": 1.0 + nrm(ks[16], (DEPTH, D_MODEL), 0.02),
        "w_router": nrm(ks[17], (DEPTH, D_MODEL, N_EXPERTS), D_MODEL ** -0.5),
        "b_router": nrm(ks[18], (DEPTH, N_EXPERTS), 0.01),
        "w_gate_up": nrm(ks[19], (DEPTH, N_EXPERTS, D_MODEL, 2 * D_EXPERT), D_MODEL ** -0.5),
        "b_gate_up": nrm(ks[20], (DEPTH, N_EXPERTS, 2 * D_EXPERT), 0.01),
        "w_down": nrm(ks[21], (DEPTH, N_EXPERTS, D_EXPERT, D_MODEL), D_EXPERT ** -0.5),
        "b_down": nrm(ks[22], (DEPTH, N_EXPERTS, D_MODEL), 0.01),
        "norm_pl_g": 1.0 + nrm(ks[23], (DEPTH, D_MODEL), 0.02),
        "w_pl_gate": nrm(ks[24], (DEPTH, D_MODEL, D_MODEL), D_MODEL ** -0.5),
        "b_pl_gate": nrm(ks[25], (DEPTH, D_MODEL), 0.01),
        "w_pl_proj": nrm(ks[26], (DEPTH, PL_DIM, D_MODEL), PL_DIM ** -0.5),
        "norm_pl_post_g": 1.0 + nrm(ks[27], (DEPTH, D_MODEL), 0.02),
        "norm_final_g": 1.0 + nrm(ks[28], (D_MODEL,), 0.02),
    }


def reference(x, p, norm_mix_g, w_in, conv_w, conv_b, lru_wa, lru_ba, lru_wx, lru_bx, lru_lambda,
              w_att_out, w_lru_out, w_merge, b_merge, w_out, norm_moe_g, w_router, b_router,
              w_gate_up, b_gate_up, w_down, b_down, norm_pl_g, w_pl_gate, b_pl_gate, w_pl_proj,
              norm_pl_post_g, norm_final_g):
    h = x
    for i in range(DEPTH):
        xn = rmsnorm(h, norm_mix_g[i])
        h = h + token_mixing(xn, w_in[i], conv_w[i], conv_b[i], lru_wa[i], lru_ba[i], lru_wx[i],
                             lru_bx[i], lru_lambda[i], w_att_out[i], w_lru_out[i], w_merge[i],
                             b_merge[i], w_out[i])
        xn = rmsnorm(h, norm_moe_g[i])
        h = h + moe(xn, w_router[i], b_router[i], w_gate_up[i], b_gate_up[i], w_down[i], b_down[i])
        hn = rmsnorm(h, norm_pl_g[i])
        pl_gate = jax.nn.sigmoid(hn @ w_pl_gate[i] + b_pl_gate[i])
        pl = rmsnorm(p[i].astype(h.dtype) @ w_pl_proj[i], norm_pl_post_g[i])
        h = h + pl_gate * pl
    return rmsnorm(h, norm_final_g)
```

```python
import functools
import math

import jax
import jax.numpy as jnp
from jax import lax
from jax.experimental import pallas as pl
from jax.experimental.pallas import tpu as pltpu

D_MODEL = 1024
N_HEADS = 8
HEAD_DIM = 64
ATT_WIDTH = N_HEADS * HEAD_DIM
LRU_WIDTH = 1024
LRU_BLOCKS = 8
LRU_BLOCK_DIM = LRU_WIDTH // LRU_BLOCKS
LRU_C = 8.0
CONV_WIDTH = 4
PL_DIM = 256
N_EXPERTS = 32
TOP_K = 4
D_EXPERT = 1024
SWIGLU_LIMIT = 7.0
SWIGLU_ALPHA = 1.702
EPS = 1e-6

LANES = 128
SUBLANES = 8
VMEM_LIMIT_BYTES = 56 * 1024 * 1024

ROW_TILE = 256
ATT_TILE = ROW_TILE
LRU_CHUNK = 256
ROUTE_TILE = 512
MOE_TILE = 256
HEAD_PAIR = LANES // HEAD_DIM

F32 = jnp.float32
BF16 = jnp.bfloat16


def _cparams(semantics):
    return pltpu.CompilerParams(dimension_semantics=semantics, vmem_limit_bytes=VMEM_LIMIT_BYTES)


def _rms(x, g):
    return x * lax.rsqrt(jnp.mean(x * x, axis=-1, keepdims=True) + EPS) * g


def _sigmoid(x):
    return 1.0 / (1.0 + jnp.exp(-x))


def _gelu_tanh(x):
    c = math.sqrt(2.0 / math.pi)
    return 0.5 * x * (1.0 + jnp.tanh(c * (x + 0.044715 * (x * x * x))))


def _dot(a, b):
    return jnp.dot(a, b, preferred_element_type=F32)


def _dot_nt(a, b):
    return lax.dot_general(a, b, (((1,), (1,)), ((), ())), preferred_element_type=F32)


def _proj_kernel(x_ref, g_ref, wqT_ref, wk_ref, wvT_ref, wu_ref, wg_ref, wm_ref, bm_ref,
                 qT_ref, k_ref, vT_ref, u_ref, gg_ref, gates_ref):
    xn = _rms(x_ref[...], g_ref[...]).astype(BF16)
    qT_ref[0] = (_dot_nt(wqT_ref[...], xn) * (1.0 / math.sqrt(HEAD_DIM))).astype(BF16)
    vT_ref[0] = _dot_nt(wvT_ref[...], xn).astype(BF16)
    k_ref[...] = _dot(xn, wk_ref[...]).astype(BF16)
    u_ref[...] = _dot(xn, wu_ref[...])
    gg_ref[...] = _gelu_tanh(_dot(xn, wg_ref[...])).astype(BF16)
    gates_ref[...] = _sigmoid(_dot(xn, wm_ref[...]) + bm_ref[...]).astype(BF16)


def _proj(x2, g, wqT, wk, wvT, wu, wg, wm, bm):
    n = x2.shape[0]
    t = ROW_TILE
    nt = n // t
    const = lambda shape: pl.BlockSpec(shape, lambda i: (0,) * len(shape))
    return pl.pallas_call(
        _proj_kernel,
        grid=(nt,),
        in_specs=[
            pl.BlockSpec((t, D_MODEL), lambda i: (i, 0)),
            const((1, D_MODEL)),
            const((ATT_WIDTH, D_MODEL)),
            const((D_MODEL, ATT_WIDTH)),
            const((ATT_WIDTH, D_MODEL)),
            const((D_MODEL, LRU_WIDTH)),
            const((D_MODEL, LRU_WIDTH)),
            const((D_MODEL, 2 * D_MODEL)),
            const((1, 2 * D_MODEL)),
        ],
        out_specs=[
            pl.BlockSpec((1, ATT_WIDTH, t), lambda i: (i, 0, 0)),
            pl.BlockSpec((t, ATT_WIDTH), lambda i: (i, 0)),
            pl.BlockSpec((1, ATT_WIDTH, t), lambda i: (i, 0, 0)),
            pl.BlockSpec((t, LRU_WIDTH), lambda i: (i, 0)),
            pl.BlockSpec((t, LRU_WIDTH), lambda i: (i, 0)),
            pl.BlockSpec((t, 2 * D_MODEL), lambda i: (i, 0)),
        ],
        out_shape=[
            jax.ShapeDtypeStruct((nt, ATT_WIDTH, t), BF16),
            jax.ShapeDtypeStruct((n, ATT_WIDTH), BF16),
            jax.ShapeDtypeStruct((nt, ATT_WIDTH, t), BF16),
            jax.ShapeDtypeStruct((n, LRU_WIDTH), F32),
            jax.ShapeDtypeStruct((n, LRU_WIDTH), BF16),
            jax.ShapeDtypeStruct((n, 2 * D_MODEL), BF16),
        ],
        compiler_params=_cparams(("arbitrary",)),
        name="proj",
    )(x2, g, wqT, wk, wvT, wu, wg, wm, bm)


def _attn_kernel(qT_ref, k_ref, vT_ref, o_ref):
    t = ATT_TILE
    qi = pl.program_id(2)
    key = lax.broadcasted_iota(jnp.int32, (t, t), 0)
    qry = lax.broadcasted_iota(jnp.int32, (t, t), 1)
    tri = jnp.where(qry > key, 1.0, 0.0).astype(BF16)
    causal = key < qry
    ones = jnp.ones((SUBLANES, t), BF16)
    head_of_row = lax.broadcasted_iota(jnp.int32, (LANES, t), 0) // HEAD_DIM
    q2 = qT_ref[0]
    qh = [jnp.where(head_of_row == h, q2, jnp.zeros_like(q2)) for h in range(HEAD_PAIR)]

    def tile(kb, h, carry, acc, diag):
        kblk = k_ref[pl.ds(pl.multiple_of(kb * t, t), t), :]
        z = _dot(kblk, qh[h])
        lk = -(jnp.maximum(z, 0.0) + jnp.log(1.0 + jnp.exp(-jnp.abs(z))))
        if diag:
            lk = jnp.where(causal, lk, 0.0)
        lkb = lk.astype(BF16)
        later = _dot(tri, lkb) + carry[0:1, :]
        w = jnp.exp(z + lk + later)
        if diag:
            w = jnp.where(causal, w, 0.0)
        vblk = vT_ref[kb, h * HEAD_DIM:(h + 1) * HEAD_DIM, :]
        acc = acc + _dot(vblk, w.astype(BF16))
        carry = carry + _dot(ones, lkb)
        return carry, acc

    state = []
    for h in range(HEAD_PAIR):
        state.append(tile(qi, h, jnp.zeros((SUBLANES, t), F32), jnp.zeros((HEAD_DIM, t), F32), True))

    def body(i, st):
        kb = qi - 1 - i
        return tuple(tile(kb, h, st[h][0], st[h][1], False) for h in range(HEAD_PAIR))

    state = lax.fori_loop(0, qi, body, tuple(state))
    accT = jnp.concatenate([state[h][1] for h in range(HEAD_PAIR)], axis=0)
    o_ref[...] = accT.T.astype(BF16)


def _attention(qT3, k2, vT3, batch, seq):
    t = ATT_TILE
    nq = seq // t
    hp = N_HEADS // HEAD_PAIR
    n = batch * seq
    return pl.pallas_call(
        _attn_kernel,
        grid=(batch, hp, nq),
        in_specs=[
            pl.BlockSpec((1, LANES, t), lambda b, p, i: (b * nq + i, p, 0)),
            pl.BlockSpec((seq, LANES), lambda b, p, i: (b, p)),
            pl.BlockSpec((nq, LANES, t), lambda b, p, i: (b, p, 0)),
        ],
        out_specs=pl.BlockSpec((t, LANES), lambda b, p, i: (b * nq + i, p)),
        out_shape=jax.ShapeDtypeStruct((n, ATT_WIDTH), BF16),
        compiler_params=_cparams(("arbitrary", "arbitrary", "arbitrary")),
        name="attn",
    )(qT3, k2, vT3)


def _lru_kernel(u_ref, gg_ref, cw_ref, cb_ref, wax_ref, ba_ref, bx_ref, lam_ref, o_ref,
                ubuf, hprev):
    t = LRU_CHUNK
    c = pl.program_id(1)

    @pl.when(c == 0)
    def _():
        ubuf[0:SUBLANES, :] = jnp.zeros((SUBLANES, LRU_WIDTH), F32)
        hprev[...] = jnp.zeros_like(hprev)

    ubuf[SUBLANES:SUBLANES + t, :] = u_ref[...]
    uc = cb_ref[...]
    for j in range(CONV_WIDTH):
        off = SUBLANES - (CONV_WIDTH - 1) + j
        uc = uc + ubuf[off:off + t, :] * cw_ref[j:j + 1, :]
    ubuf[0:SUBLANES, :] = ubuf[t:t + SUBLANES, :]

    ucb = uc.astype(BF16)
    rs, is_ = [], []
    for nb in range(LRU_BLOCKS):
        sl = slice(nb * LRU_BLOCK_DIM, (nb + 1) * LRU_BLOCK_DIM)
        ri = _dot(ucb[:, sl], wax_ref[nb])
        rs.append(ri[:, :LRU_BLOCK_DIM])
        is_.append(ri[:, LRU_BLOCK_DIM:])
    r = _sigmoid(jnp.concatenate(rs, axis=1) + ba_ref[...])
    ig = _sigmoid(jnp.concatenate(is_, axis=1) + bx_ref[...])
    lam = lam_ref[...]
    log_sig_lam = -(jnp.maximum(-lam, 0.0) + jnp.log(1.0 + jnp.exp(-jnp.abs(lam))))
    a = jnp.exp((LRU_C * r) * log_sig_lam)
    b = jnp.sqrt(1.0 - a * a) * (ig * uc)

    row = lax.broadcasted_iota(jnp.int32, (t, LRU_WIDTH), 0)
    d = 1
    while d < t:
        valid = row >= d
        a_sh = pltpu.roll(a, d, axis=0)
        b_sh = pltpu.roll(b, d, axis=0)
        b = b + jnp.where(valid, a * b_sh, 0.0)
        a = jnp.where(valid, a * a_sh, a)
        d *= 2
    h = a * hprev[0:1, :] + b
    hprev[...] = jnp.broadcast_to(h[t - 1:t, :], hprev.shape)
    o_ref[...] = (gg_ref[...].astype(F32) * h).astype(BF16)


def _lru(u2, gg2, conv_w, conv_b, wax, ba, bx, lam, batch, seq):
    t = LRU_CHUNK
    nc = seq // t
    n = batch * seq
    const = lambda shape: pl.BlockSpec(shape, lambda b, c: (0,) * len(shape))
    return pl.pallas_call(
        _lru_kernel,
        grid=(batch, nc),
        in_specs=[
            pl.BlockSpec((t, LRU_WIDTH), lambda b, c: (b * nc + c, 0)),
            pl.BlockSpec((t, LRU_WIDTH), lambda b, c: (b * nc + c, 0)),
            const((CONV_WIDTH, LRU_WIDTH)),
            const((1, LRU_WIDTH)),
            const((LRU_BLOCKS, LRU_BLOCK_DIM, 2 * LRU_BLOCK_DIM)),
            const((1, LRU_WIDTH)),
            const((1, LRU_WIDTH)),
            const((1, LRU_WIDTH)),
        ],
        out_specs=pl.BlockSpec((t, LRU_WIDTH), lambda b, c: (b * nc + c, 0)),
        out_shape=jax.ShapeDtypeStruct((n, LRU_WIDTH), BF16),
        scratch_shapes=[
            pltpu.VMEM((t + SUBLANES, LRU_WIDTH), F32),
            pltpu.VMEM((SUBLANES, LRU_WIDTH), F32),
        ],
        compiler_params=_cparams(("arbitrary", "arbitrary")),
        name="lru",
    )(u2, gg2, conv_w, conv_b, wax, ba, bx, lam)


def _postmix_kernel(x_ref, att_ref, ml_ref, gates_ref, wa_ref, wl_ref, wo_ref, gn_ref,
                    wrT_ref, br_ref, h1_ref, xn_ref, lgT_ref):
    y_att = _dot(att_ref[...], wa_ref[...])
    y_lru = _dot(ml_ref[...], wl_ref[...])
    gates = gates_ref[...].astype(F32)
    merged = gates[:, :D_MODEL] * y_att + gates[:, D_MODEL:] * y_lru
    h1 = x_ref[...] + _dot(merged.astype(BF16), wo_ref[...])
    h1_ref[...] = h1
    xn = _rms(h1, gn_ref[...])
    xn_ref[...] = xn
    lgT_ref[...] = lax.dot_general(wrT_ref[...], xn, (((1,), (1,)), ((), ())),
                                   precision=lax.Precision.HIGHEST,
                                   preferred_element_type=F32) + br_ref[...]


def _postmix(x2, att, ml, gates, wa, wl, wo, gn, wrT, br_col):
    n = x2.shape[0]
    t = ROW_TILE
    const = lambda shape: pl.BlockSpec(shape, lambda i: (0,) * len(shape))
    return pl.pallas_call(
        _postmix_kernel,
        grid=(n // t,),
        in_specs=[
            pl.BlockSpec((t, D_MODEL), lambda i: (i, 0)),
            pl.BlockSpec((t, ATT_WIDTH), lambda i: (i, 0)),
            pl.BlockSpec((t, LRU_WIDTH), lambda i: (i, 0)),
            pl.BlockSpec((t, 2 * D_MODEL), lambda i: (i, 0)),
            const((ATT_WIDTH, D_MODEL)),
            const((LRU_WIDTH, D_MODEL)),
            const((D_MODEL, D_MODEL)),
            const((1, D_MODEL)),
            const((N_EXPERTS, D_MODEL)),
            const((N_EXPERTS, 1)),
        ],
        out_specs=[
            pl.BlockSpec((t, D_MODEL), lambda i: (i, 0)),
            pl.BlockSpec((t, D_MODEL), lambda i: (i, 0)),
            pl.BlockSpec((N_EXPERTS, t), lambda i: (0, i)),
        ],
        out_shape=[
            jax.ShapeDtypeStruct((n, D_MODEL), F32),
            jax.ShapeDtypeStruct((n, D_MODEL), F32),
            jax.ShapeDtypeStruct((N_EXPERTS, n), F32),
        ],
        compiler_params=_cparams(("arbitrary",)),
        name="postmix",
    )(x2, att, ml, gates, wa, wl, wo, gn, wrT, br_col)


def _top4(lg):
    e_iota = lax.broadcasted_iota(jnp.int32, lg.shape, 0).astype(F32)
    picks = []
    for _ in range(TOP_K):
        m = jnp.max(lg, axis=0, keepdims=True)
        first = jnp.min(jnp.where(lg == m, e_iota, float(N_EXPERTS)), axis=0, keepdims=True)
        sel = e_iota == first
        picks.append((sel, m))
        lg = jnp.where(sel, -jnp.inf, lg)
    return picks


def _count_kernel(lgT_ref, cnt_ref, acc):
    i = pl.program_id(0)

    @pl.when(i == 0)
    def _():
        acc[...] = jnp.zeros_like(acc)

    picks = _top4(lgT_ref[...])
    onehot = sum(jnp.where(sel, 1.0, 0.0) for sel, _ in picks)
    part = onehot[:, 0:LANES]
    for c in range(1, ROUTE_TILE // LANES):
        part = part + onehot[:, c * LANES:(c + 1) * LANES]
    acc[...] += part

    @pl.when(i == pl.num_programs(0) - 1)
    def _():
        cnt_ref[...] = jnp.broadcast_to(jnp.sum(acc[...], axis=1, keepdims=True), cnt_ref.shape)


def _count(lgT):
    n = lgT.shape[1]
    return pl.pallas_call(
        _count_kernel,
        grid=(n // ROUTE_TILE,),
        in_specs=[pl.BlockSpec((N_EXPERTS, ROUTE_TILE), lambda i: (0, i))],
        out_specs=pl.BlockSpec((N_EXPERTS, LANES), lambda i: (0, 0)),
        out_shape=jax.ShapeDtypeStruct((N_EXPERTS, LANES), F32),
        scratch_shapes=[pltpu.VMEM((N_EXPERTS, LANES), F32)],
        compiler_params=_cparams(("arbitrary",)),
        name="count",
    )(lgT)


def _route_kernel(lgT_ref, cnt_ref, dest_ref, gcol_ref, bexp_ref, nact_ref, zrow_ref, base, pstart,
                  *, n_blocks_pad):
    i = pl.program_id(0)
    t = ROUTE_TILE

    @pl.when(i == 0)
    def _():
        cnt = cnt_ref[...]
        nblk = jnp.floor((cnt + (MOE_TILE - 1)) * (1.0 / MOE_TILE))
        er = lax.broadcasted_iota(jnp.int32, (N_EXPERTS, N_EXPERTS), 0)
        ec = lax.broadcasted_iota(jnp.int32, (N_EXPERTS, N_EXPERTS), 1)
        lower = jnp.where(ec < er, 1.0, 0.0).astype(BF16)
        blk_start = _dot(lower, nblk.astype(BF16))
        blk_end = blk_start + nblk
        pstart[...] = blk_start * MOE_TILE
        base[...] = jnp.zeros_like(base)
        blk = lax.broadcasted_iota(jnp.int32, (N_EXPERTS, n_blocks_pad), 1).astype(F32)
        ended = jnp.where(blk_end[:, 0:1] <= blk, 1.0, 0.0)
        bexp = jnp.sum(ended, axis=0, keepdims=True)
        bexp_ref[...] = jnp.minimum(bexp, N_EXPERTS - 1).astype(jnp.int32)
        nact_ref[...] = jnp.broadcast_to(blk_end[N_EXPERTS - 1:N_EXPERTS, :], nact_ref.shape).astype(jnp.int32)
        zr = jnp.where(nblk > 0.0, (blk_end - 1.0) * MOE_TILE, -1.0)
        zrow_ref[...] = zr.astype(jnp.int32)

    picks = _top4(lgT_ref[...])
    onehot = sum(jnp.where(sel, 1.0, 0.0) for sel, _ in picks)
    tr = lax.broadcasted_iota(jnp.int32, (t, t), 0)
    tc = lax.broadcasted_iota(jnp.int32, (t, t), 1)
    before = jnp.where(tr < tc, 1.0, 0.0).astype(BF16)
    prefix = _dot(onehot.astype(BF16), before)
    slot = pstart[:, 0:1] + base[:, 0:1] + prefix
    krow = lax.broadcasted_iota(jnp.int32, (SUBLANES, t), 0)
    dest = jnp.zeros((SUBLANES, t), F32)
    for k, (sel, _) in enumerate(picks):
        dest = jnp.where(krow == k, jnp.sum(jnp.where(sel, slot, 0.0), axis=0, keepdims=True), dest)
    dest_ref[...] = dest.astype(jnp.int32)
    part = onehot[:, 0:LANES]
    for c in range(1, t // LANES):
        part = part + onehot[:, c * LANES:(c + 1) * LANES]
    base[...] += jnp.broadcast_to(jnp.sum(part, axis=1, keepdims=True), base.shape)

    vmax = picks[0][1]
    ex = [jnp.exp(v - vmax) for _, v in picks]
    inv = 1.0 / sum(ex)
    grow = lax.broadcasted_iota(jnp.int32, (LANES, t), 0)
    g = jnp.zeros((LANES, t), F32)
    for k, e in enumerate(ex):
        g = jnp.where(grow == k, e * inv, g)
    gcol_ref[...] = g.T


def _route(lgT, cnt, n_blocks_pad):
    n = lgT.shape[1]
    t = ROUTE_TILE
    return pl.pallas_call(
        functools.partial(_route_kernel, n_blocks_pad=n_blocks_pad),
        grid=(n // t,),
        in_specs=[
            pl.BlockSpec((N_EXPERTS, t), lambda i: (0, i)),
            pl.BlockSpec((N_EXPERTS, LANES), lambda i: (0, 0)),
        ],
        out_specs=[
            pl.BlockSpec((SUBLANES, t), lambda i: (0, i)),
            pl.BlockSpec((t, LANES), lambda i: (i, 0)),
            pl.BlockSpec((1, n_blocks_pad), lambda i: (0, 0)),
            pl.BlockSpec((1, LANES), lambda i: (0, 0)),
            pl.BlockSpec((N_EXPERTS, LANES), lambda i: (0, 0)),
        ],
        out_shape=[
            jax.ShapeDtypeStruct((SUBLANES, n), jnp.int32),
            jax.ShapeDtypeStruct((n, LANES), F32),
            jax.ShapeDtypeStruct((1, n_blocks_pad), jnp.int32),
            jax.ShapeDtypeStruct((1, LANES), jnp.int32),
            jax.ShapeDtypeStruct((N_EXPERTS, LANES), jnp.int32),
        ],
        scratch_shapes=[pltpu.VMEM((N_EXPERTS, LANES), F32), pltpu.VMEM((N_EXPERTS, LANES), F32)],
        compiler_params=_cparams(("arbitrary",)),
        name="route",
    )(lgT, cnt)


DISPATCH_TILE = 256


def _dispatch_kernel(nact_ref, zrow_ref, dest_ref, xn_hbm, xs_hbm, zbuf, sem, zsem, *, n_blocks):
    i = pl.program_id(0)
    t = DISPATCH_TILE

    def zero_copy(row):
        return pltpu.make_async_copy(zbuf, xs_hbm.at[pl.ds(pl.multiple_of(row, MOE_TILE), MOE_TILE)], zsem)

    @pl.when(i == 0)
    def _():
        zbuf[...] = jnp.zeros_like(zbuf)
        nact = nact_ref[0]
        for phase in ("start", "wait"):
            for e in range(N_EXPERTS):
                row = zrow_ref[e]
                tail = (nact + e) * MOE_TILE

                @pl.when(row >= 0)
                def _():
                    getattr(zero_copy(row), phase)()

                @pl.when(nact + e < n_blocks)
                def _():
                    getattr(zero_copy(tail), phase)()

    def row_copy(src_row, dst_row):
        return pltpu.make_async_copy(xn_hbm.at[pl.ds(src_row, 1)], xs_hbm.at[pl.ds(dst_row, 1)], sem)

    def start(j, c):
        for k in range(TOP_K):
            row_copy(i * t + j, dest_ref[k, j]).start()
        return c

    def wait(j, c):
        for k in range(TOP_K):
            row_copy(0, 0).wait()
        return c

    lax.fori_loop(0, t, start, 0, unroll=8)
    lax.fori_loop(0, t, wait, 0, unroll=8)


def _dispatch(nact, zrow, dest, xn, n_blocks):
    n = xn.shape[0]
    t = DISPATCH_TILE
    return pl.pallas_call(
        functools.partial(_dispatch_kernel, n_blocks=n_blocks),
        grid=(n // t,),
        in_specs=[
            pl.BlockSpec(memory_space=pltpu.SMEM),
            pl.BlockSpec(memory_space=pltpu.SMEM),
            pl.BlockSpec((SUBLANES, t), lambda i: (0, i), memory_space=pltpu.SMEM),
            pl.BlockSpec(memory_space=pl.ANY),
        ],
        out_specs=pl.BlockSpec(memory_space=pl.ANY),
        out_shape=jax.ShapeDtypeStruct((n_blocks * MOE_TILE, D_MODEL), F32),
        scratch_shapes=[
            pltpu.VMEM((MOE_TILE, D_MODEL), F32),
            pltpu.SemaphoreType.DMA(()),
            pltpu.SemaphoreType.DMA(()),
        ],
        compiler_params=_cparams(("arbitrary",)),
        name="dispatch",
    )(nact, zrow, dest, xn)


def _experts_kernel(bexp_ref, nact_ref, xs_ref, wgu_ref, bgu_ref, wdn_ref, bdn_ref, ys_ref):
    i = pl.program_id(0)

    @pl.when(i < nact_ref[0])
    def _():
        x = xs_ref[...].astype(BF16)
        hgu = _dot(x, wgu_ref[0]) + bgu_ref[0]
        gate = jnp.minimum(hgu[:, :D_EXPERT], SWIGLU_LIMIT)
        up = jnp.clip(hgu[:, D_EXPERT:], -SWIGLU_LIMIT, SWIGLU_LIMIT)
        glu = gate * _sigmoid(gate * SWIGLU_ALPHA)
        act = ((up + 1.0) * glu).astype(BF16)
        ys_ref[...] = _dot(act, wdn_ref[0]) + bdn_ref[0]

    @pl.when(i >= nact_ref[0])
    def _():
        ys_ref[...] = jnp.zeros_like(ys_ref)


def _experts(bexp, nact, xs, wgu, bgu, wdn, bdn, n_blocks):
    def blk(i, be, na):
        return jnp.minimum(i, na[0] - 1)

    grid_spec = pltpu.PrefetchScalarGridSpec(
        num_scalar_prefetch=2,
        grid=(n_blocks,),
        in_specs=[
            pl.BlockSpec((MOE_TILE, D_MODEL), lambda i, be, na: (blk(i, be, na), 0)),
            pl.BlockSpec((1, D_MODEL, 2 * D_EXPERT), lambda i, be, na: (be[blk(i, be, na)], 0, 0)),
            pl.BlockSpec((1, 1, 2 * D_EXPERT), lambda i, be, na: (be[blk(i, be, na)], 0, 0)),
            pl.BlockSpec((1, D_EXPERT, D_MODEL), lambda i, be, na: (be[blk(i, be, na)], 0, 0)),
            pl.BlockSpec((1, 1, D_MODEL), lambda i, be, na: (be[blk(i, be, na)], 0, 0)),
        ],
        out_specs=pl.BlockSpec((MOE_TILE, D_MODEL), lambda i, be, na: (i, 0)),
    )
    return pl.pallas_call(
        _experts_kernel,
        grid_spec=grid_spec,
        out_shape=jax.ShapeDtypeStruct((n_blocks * MOE_TILE, D_MODEL), F32),
        compiler_params=_cparams(("arbitrary",)),
        name="experts",
    )(bexp, nact, xs, wgu, bgu, wdn, bdn)


FINAL_TILE = 256


def _final_kernel(dest_ref, ys_hbm, h1_ref, gcol_ref, p_ref, gpl_ref, wpg_ref, bpg_ref, wpp_ref,
                  gpost_ref, gfin_ref, o_ref, rows, sem):
    i = pl.program_id(0)
    t = FINAL_TILE

    def row_copy(src_row, k, j):
        return pltpu.make_async_copy(ys_hbm.at[pl.ds(src_row, 1)], rows.at[k, pl.ds(j, 1)], sem)

    def start(j, c):
        for k in range(TOP_K):
            row_copy(dest_ref[k, j], k, j).start()
        return c

    def wait(j, c):
        for k in range(TOP_K):
            row_copy(0, 0, 0).wait()
        return c

    lax.fori_loop(0, t, start, 0, unroll=8)
    lax.fori_loop(0, t, wait, 0, unroll=8)

    g = gcol_ref[...]
    moe = g[:, 0:1] * rows[0]
    for k in range(1, TOP_K):
        moe = moe + g[:, k:k + 1] * rows[k]
    h2 = h1_ref[...] + moe
    hn = _rms(h2, gpl_ref[...]).astype(BF16)
    pl_gate = _sigmoid(_dot(hn, wpg_ref[...]) + bpg_ref[...])
    pl_in = _rms(_dot(p_ref[...].astype(BF16), wpp_ref[...]), gpost_ref[...])
    h3 = h2 + pl_gate * pl_in
    o_ref[...] = _rms(h3, gfin_ref[...])


def _final(dest, ys, h1, gcol, p2, gpl, wpg, bpg, wpp, gpost, gfin):
    n = h1.shape[0]
    t = FINAL_TILE
    const = lambda shape: pl.BlockSpec(shape, lambda i: (0,) * len(shape))
    return pl.pallas_call(
        _final_kernel,
        grid=(n // t,),
        in_specs=[
            pl.BlockSpec((SUBLANES, t), lambda i: (0, i), memory_space=pltpu.SMEM),
            pl.BlockSpec(memory_space=pl.ANY),
            pl.BlockSpec((t, D_MODEL), lambda i: (i, 0)),
            pl.BlockSpec((t, LANES), lambda i: (i, 0)),
            pl.BlockSpec((t, PL_DIM), lambda i: (i, 0)),
            const((1, D_MODEL)),
            const((D_MODEL, D_MODEL)),
            const((1, D_MODEL)),
            const((PL_DIM, D_MODEL)),
            const((1, D_MODEL)),
            const((1, D_MODEL)),
        ],
        out_specs=pl.BlockSpec((t, D_MODEL), lambda i: (i, 0)),
        out_shape=jax.ShapeDtypeStruct((n, D_MODEL), F32),
        scratch_shapes=[pltpu.VMEM((TOP_K, t, D_MODEL), F32), pltpu.SemaphoreType.DMA(())],
        compiler_params=_cparams(("arbitrary",)),
        name="final",
    )(dest, ys, h1, gcol, p2, gpl, wpg, bpg, wpp, gpost, gfin)


def _layer(h, p_i, norm_mix_g, w_in, conv_w, conv_b, lru_wa, lru_ba, lru_wx, lru_bx, lru_lambda,
           w_att_out, w_lru_out, w_merge, b_merge, w_out, norm_moe_g, w_router, b_router,
           w_gate_up, b_gate_up, w_down, b_down, norm_pl_g, w_pl_gate, b_pl_gate, w_pl_proj,
           norm_pl_post_g, norm_final_g):
    batch, seq, _ = h.shape
    n = batch * seq
    x2 = h.reshape(n, D_MODEL)
    row = lambda v: v.reshape(1, -1)
    a = ATT_WIDTH
    wq, wk, wv = w_in[:, :a], w_in[:, a:2 * a], w_in[:, 2 * a:3 * a]
    wu, wg = w_in[:, 3 * a:3 * a + LRU_WIDTH], w_in[:, 3 * a + LRU_WIDTH:]
    qT3, k2, vT3, u2, gg2, gates = _proj(
        x2, row(norm_mix_g), wq.T.astype(BF16), wk.astype(BF16), wv.T.astype(BF16),
        wu.astype(BF16), wg.astype(BF16), w_merge.astype(BF16), row(b_merge))
    att = _attention(qT3, k2, vT3, batch, seq)
    wax = jnp.concatenate([lru_wa, lru_wx], axis=-1).astype(BF16)
    ml = _lru(u2, gg2, conv_w, row(conv_b), wax, row(lru_ba), row(lru_bx), row(lru_lambda), batch, seq)
    h1, xn2, lgT = _postmix(x2, att, ml, gates, w_att_out.astype(BF16), w_lru_out.astype(BF16),
                            w_out.astype(BF16), row(norm_moe_g), w_router.T, b_router.reshape(-1, 1))
    n_blocks = (n * TOP_K) // MOE_TILE + N_EXPERTS
    n_blocks_pad = -(-n_blocks // LANES) * LANES
    cnt = _count(lgT)
    dest, gcol, bexp, nact, zrow = _route(lgT, cnt, n_blocks_pad)
    nact1 = nact[0, :1]
    xs = _dispatch(nact1, zrow[:, 0], dest, xn2, n_blocks)
    ys = _experts(bexp[0], nact1, xs, w_gate_up.astype(BF16), b_gate_up[:, None, :],
                  w_down.astype(BF16), b_down[:, None, :], n_blocks)
    out = _final(dest, ys, h1, gcol, p_i.reshape(n, PL_DIM), row(norm_pl_g), w_pl_gate.astype(BF16),
                 row(b_pl_gate), w_pl_proj.astype(BF16), row(norm_pl_post_g), row(norm_final_g))
    return out.reshape(batch, seq, D_MODEL)


def kernel(x, p, norm_mix_g, w_in, conv_w, conv_b, lru_wa, lru_ba, lru_wx, lru_bx, lru_lambda, w_att_out, w_lru_out, w_merge, b_merge, w_out, norm_moe_g, w_router, b_router, w_gate_up, b_gate_up, w_down, b_down, norm_pl_g, w_pl_gate, b_pl_gate, w_pl_proj, norm_pl_post_g, norm_final_g):
    depth = p.shape[0]
    assert depth == 1, "single-layer trunk: the final norm is fused into the layer's last kernel"
    i = 0
    return _layer(x, p[i], norm_mix_g[i], w_in[i], conv_w[i], conv_b[i], lru_wa[i], lru_ba[i],
                  lru_wx[i], lru_bx[i], lru_lambda[i], w_att_out[i], w_lru_out[i], w_merge[i],
                  b_merge[i], w_out[i], norm_moe_g[i], w_router[i], b_router[i], w_gate_up[i],
                  b_gate_up[i], w_down[i], b_down[i], norm_pl_g[i], w_pl_gate[i], b_pl_gate[i],
                  w_pl_proj[i], norm_pl_post_g[i], norm_final_g)
```

```python
import functools
import math

import jax
import jax.numpy as jnp
from jax import lax
from jax.experimental import pallas as pl
from jax.experimental.pallas import tpu as pltpu

D_MODEL = 1024
N_HEADS = 8
HEAD_DIM = 64
ATT_WIDTH = N_HEADS * HEAD_DIM
LRU_WIDTH = 1024
LRU_BLOCKS = 8
LRU_BLOCK_DIM = LRU_WIDTH // LRU_BLOCKS
LRU_C = 8.0
CONV_WIDTH = 4
PL_DIM = 256
N_EXPERTS = 32
TOP_K = 4
D_EXPERT = 1024
SWIGLU_LIMIT = 7.0
SWIGLU_ALPHA = 1.702
EPS = 1e-6

LANES = 128
SUBLANES = 8
VMEM_LIMIT_BYTES = 56 * 1024 * 1024

ROW_TILE = 256
ATT_TILE = ROW_TILE
LRU_CHUNK = 256
ROUTE_TILE = 512
MOE_TILE = 256
HEAD_PAIR = LANES // HEAD_DIM

F32 = jnp.float32
BF16 = jnp.bfloat16


def _cparams(semantics):
    return pltpu.CompilerParams(dimension_semantics=semantics, vmem_limit_bytes=VMEM_LIMIT_BYTES)


def _rms(x, g):
    return x * lax.rsqrt(jnp.mean(x * x, axis=-1, keepdims=True) + EPS) * g


def _sigmoid(x):
    return 1.0 / (1.0 + jnp.exp(-x))


def _gelu_tanh(x):
    c = math.sqrt(2.0 / math.pi)
    return 0.5 * x * (1.0 + jnp.tanh(c * (x + 0.044715 * (x * x * x))))


def _dot(a, b):
    return jnp.dot(a, b, preferred_element_type=F32)


def _dot_nt(a, b):
    return lax.dot_general(a, b, (((1,), (1,)), ((), ())), preferred_element_type=F32)


def _proj_kernel(x_ref, g_ref, wqT_ref, wk_ref, wvT_ref, wu_ref, wg_ref, wm_ref, bm_ref,
                 qT_ref, k_ref, vT_ref, u_ref, gg_ref, gates_ref):
    xn = _rms(x_ref[...], g_ref[...]).astype(BF16)
    qT_ref[0] = (_dot_nt(wqT_ref[...], xn) * (1.0 / math.sqrt(HEAD_DIM))).astype(BF16)
    vT_ref[0] = _dot_nt(wvT_ref[...], xn).astype(BF16)
    k_ref[...] = _dot(xn, wk_ref[...]).astype(BF16)
    u_ref[...] = _dot(xn, wu_ref[...])
    gg_ref[...] = _gelu_tanh(_dot(xn, wg_ref[...])).astype(BF16)
    gates_ref[...] = _sigmoid(_dot(xn, wm_ref[...]) + bm_ref[...]).astype(BF16)


def _proj(x2, g, wqT, wk, wvT, wu, wg, wm, bm):
    n = x2.shape[0]
    t = ROW_TILE
    nt = n // t
    const = lambda shape: pl.BlockSpec(shape, lambda i: (0,) * len(shape))
    return pl.pallas_call(
        _proj_kernel,
        grid=(nt,),
        in_specs=[
            pl.BlockSpec((t, D_MODEL), lambda i: (i, 0)),
            const((1, D_MODEL)),
            const((ATT_WIDTH, D_MODEL)),
            const((D_MODEL, ATT_WIDTH)),
            const((ATT_WIDTH, D_MODEL)),
            const((D_MODEL, LRU_WIDTH)),
            const((D_MODEL, LRU_WIDTH)),
            const((D_MODEL, 2 * D_MODEL)),
            const((1, 2 * D_MODEL)),
        ],
        out_specs=[
            pl.BlockSpec((1, ATT_WIDTH, t), lambda i: (i, 0, 0)),
            pl.BlockSpec((t, ATT_WIDTH), lambda i: (i, 0)),
            pl.BlockSpec((1, ATT_WIDTH, t), lambda i: (i, 0, 0)),
            pl.BlockSpec((t, LRU_WIDTH), lambda i: (i, 0)),
            pl.BlockSpec((t, LRU_WIDTH), lambda i: (i, 0)),
            pl.BlockSpec((t, 2 * D_MODEL), lambda i: (i, 0)),
        ],
        out_shape=[
            jax.ShapeDtypeStruct((nt, ATT_WIDTH, t), BF16),
            jax.ShapeDtypeStruct((n, ATT_WIDTH), BF16),
            jax.ShapeDtypeStruct((nt, ATT_WIDTH, t), BF16),
            jax.ShapeDtypeStruct((n, LRU_WIDTH), F32),
            jax.ShapeDtypeStruct((n, LRU_WIDTH), BF16),
            jax.ShapeDtypeStruct((n, 2 * D_MODEL), BF16),
        ],
        compiler_params=_cparams(("arbitrary",)),
        name="proj",
    )(x2, g, wqT, wk, wvT, wu, wg, wm, bm)


ATT_KT = ATT_TILE
ATT_QT = ATT_TILE
ATT_KGROUP = ATT_QT // ATT_KT
ATT_PAIRS = 4
ATT_HEADS = ATT_PAIRS * HEAD_PAIR
ATT_LANES = ATT_PAIRS * LANES
ATT_EXIT = 110.0


def _attn_kernel(qT_ref, k_ref, vT_ref, o_ref):
    kt, qt = ATT_KT, ATT_QT
    qi = pl.program_id(2)
    key = lax.broadcasted_iota(jnp.int32, (kt, qt), 0)
    qry = lax.broadcasted_iota(jnp.int32, (kt, qt), 1)
    sq_r = lax.broadcasted_iota(jnp.int32, (kt, kt), 0)
    sq_c = lax.broadcasted_iota(jnp.int32, (kt, kt), 1)
    tri = jnp.where(sq_c > sq_r, 1.0, 0.0).astype(BF16)
    ones = jnp.ones((SUBLANES, kt), BF16)
    head_of_row = lax.broadcasted_iota(jnp.int32, (LANES, qt), 0) // HEAD_DIM
    q_all = jnp.concatenate([qT_ref[j] for j in range(ATT_KGROUP)], axis=1)
    qh = []
    for h in range(ATT_HEADS):
        pair = h // HEAD_PAIR
        q2 = q_all[pair * LANES:(pair + 1) * LANES, :]
        qh.append(jnp.where(head_of_row == h % HEAD_PAIR, q2, jnp.zeros_like(q2)))

    def group(kb0, carries, accs, diag):
        order = [(h, g) for h in range(ATT_HEADS) for g in reversed(range(ATT_KGROUP))]
        zs, lsig, spbs = {}, {}, {}
        kblk = {g: k_ref[pl.ds(pl.multiple_of((kb0 + g) * kt, kt), kt), :] for g in range(ATT_KGROUP)}
        for c in order:
            pair = c[0] // HEAD_PAIR
            zs[c] = _dot(kblk[c[1]][:, pair * LANES:(pair + 1) * LANES], qh[c[0]])
        for c in order:
            z = zs[c]
            neg_abs = pltpu.bitcast(pltpu.bitcast(z, jnp.uint32) | jnp.uint32(0x80000000), F32)
            sp = jnp.maximum(z, 0.0) + jnp.log(1.0 + jnp.exp(neg_abs))
            if diag:
                sp = jnp.where(key + c[1] * kt < qry, sp, 0.0)
            lsig[c] = z - sp
            spbs[c] = sp.astype(BF16)
        cums = {c: _dot(tri, spbs[c]) for c in order}
        sums = {c: _dot(ones, spbs[c]) for c in order}
        ws = {}
        carries = list(carries)
        for c in order:
            h = c[0]
            later = cums[c] + carries[h][0:1, :]
            w = jnp.exp(lsig[c] - later)
            if diag:
                w = jnp.where(key + c[1] * kt < qry, w, 0.0)
            ws[c] = w.astype(BF16)
            carries[h] = carries[h] + sums[c]
        accs = list(accs)
        for c in order:
            h, g = c
            vblk = vT_ref[kb0 + g, h * HEAD_DIM:(h + 1) * HEAD_DIM, :]
            accs[h] = accs[h] + _dot(vblk, ws[c])
        return tuple(carries), tuple(accs)

    zero_c = tuple(jnp.zeros((SUBLANES, qt), F32) for _ in range(ATT_HEADS))
    zero_a = tuple(jnp.zeros((HEAD_DIM, qt), F32) for _ in range(ATT_HEADS))
    state = group(qi * ATT_KGROUP, zero_c, zero_a, True)

    def cond(st):
        i, carries, _ = st
        smallest = functools.reduce(jnp.minimum, [jnp.min(c) for c in carries])
        return jnp.logical_and(i < qi, smallest < ATT_EXIT)

    def body(st):
        i, carries, accs = st
        carries, accs = group((qi - 1 - i) * ATT_KGROUP, carries, accs, False)
        return i + 1, carries, accs

    _, _, accs = lax.while_loop(cond, body, (jnp.int32(0),) + state)
    accT = jnp.concatenate(list(accs), axis=0)
    o_ref[...] = accT.T.astype(BF16)


def _attention(qT3, k2, vT3, batch, seq):
    kt, qt = ATT_KT, ATT_QT
    nq = seq // qt
    nk = seq // kt
    hp = N_HEADS // ATT_HEADS
    n = batch * seq
    return pl.pallas_call(
        _attn_kernel,
        grid=(batch, hp, nq),
        in_specs=[
            pl.BlockSpec((ATT_KGROUP, ATT_LANES, kt), lambda b, p, i: (b * nq + i, p, 0)),
            pl.BlockSpec((seq, ATT_LANES), lambda b, p, i: (b, p)),
            pl.BlockSpec((nk, ATT_LANES, kt), lambda b, p, i: (b, p, 0)),
        ],
        out_specs=pl.BlockSpec((qt, ATT_LANES), lambda b, p, i: (b * nq + i, p)),
        out_shape=jax.ShapeDtypeStruct((n, ATT_WIDTH), BF16),
        compiler_params=_cparams(("arbitrary", "arbitrary", "arbitrary")),
        name="attn",
    )(qT3, k2, vT3)


def _lru_kernel(u_ref, gg_ref, cw_ref, cb_ref, wax_ref, ba_ref, bx_ref, lam_ref, o_ref,
                ubuf, hprev):
    t = LRU_CHUNK
    c = pl.program_id(1)

    @pl.when(c == 0)
    def _():
        ubuf[0:SUBLANES, :] = jnp.zeros((SUBLANES, LRU_WIDTH), F32)
        hprev[...] = jnp.zeros_like(hprev)

    ubuf[SUBLANES:SUBLANES + t, :] = u_ref[...]
    uc = cb_ref[...]
    for j in range(CONV_WIDTH):
        off = SUBLANES - (CONV_WIDTH - 1) + j
        uc = uc + ubuf[off:off + t, :] * cw_ref[j:j + 1, :]
    ubuf[0:SUBLANES, :] = ubuf[t:t + SUBLANES, :]

    ucb = uc.astype(BF16)
    rs, is_ = [], []
    for nb in range(LRU_BLOCKS):
        sl = slice(nb * LRU_BLOCK_DIM, (nb + 1) * LRU_BLOCK_DIM)
        ri = _dot(ucb[:, sl], wax_ref[nb])
        rs.append(ri[:, :LRU_BLOCK_DIM])
        is_.append(ri[:, LRU_BLOCK_DIM:])
    r = _sigmoid(jnp.concatenate(rs, axis=1) + ba_ref[...])
    ig = _sigmoid(jnp.concatenate(is_, axis=1) + bx_ref[...])
    lam = lam_ref[...]
    log_sig_lam = -(jnp.maximum(-lam, 0.0) + jnp.log(1.0 + jnp.exp(-jnp.abs(lam))))
    a = jnp.exp((LRU_C * r) * log_sig_lam)
    b = jnp.sqrt(1.0 - a * a) * (ig * uc)

    grouped = (t // SUBLANES, SUBLANES, LRU_WIDTH)
    a = a.reshape(grouped)
    b = b.reshape(grouped)
    row = lax.broadcasted_iota(jnp.int32, grouped, 1)
    d = 1
    while d < SUBLANES:
        valid = row >= d
        a_sh = pltpu.roll(a, d, axis=1)
        b_sh = pltpu.roll(b, d, axis=1)
        b = b + jnp.where(valid, a * b_sh, 0.0)
        a = jnp.where(valid, a * a_sh, a)
        d *= 2
    a = a.reshape(t, LRU_WIDTH)
    b = b.reshape(t, LRU_WIDTH)
    state = hprev[...]
    hs = []
    for g in range(t // SUBLANES):
        rows = slice(g * SUBLANES, (g + 1) * SUBLANES)
        h = a[rows, :] * state + b[rows, :]
        hs.append(h)
        state = jnp.broadcast_to(h[SUBLANES - 1:SUBLANES, :], state.shape)
    hprev[...] = state
    o_ref[...] = (gg_ref[...].astype(F32) * jnp.concatenate(hs, axis=0)).astype(BF16)


def _lru(u2, gg2, conv_w, conv_b, wax, ba, bx, lam, batch, seq):
    t = LRU_CHUNK
    nc = seq // t
    n = batch * seq
    const = lambda shape: pl.BlockSpec(shape, lambda b, c: (0,) * len(shape))
    return pl.pallas_call(
        _lru_kernel,
        grid=(batch, nc),
        in_specs=[
            pl.BlockSpec((t, LRU_WIDTH), lambda b, c: (b * nc + c, 0)),
            pl.BlockSpec((t, LRU_WIDTH), lambda b, c: (b * nc + c, 0)),
            const((CONV_WIDTH, LRU_WIDTH)),
            const((1, LRU_WIDTH)),
            const((LRU_BLOCKS, LRU_BLOCK_DIM, 2 * LRU_BLOCK_DIM)),
            const((1, LRU_WIDTH)),
            const((1, LRU_WIDTH)),
            const((1, LRU_WIDTH)),
        ],
        out_specs=pl.BlockSpec((t, LRU_WIDTH), lambda b, c: (b * nc + c, 0)),
        out_shape=jax.ShapeDtypeStruct((n, LRU_WIDTH), BF16),
        scratch_shapes=[
            pltpu.VMEM((t + SUBLANES, LRU_WIDTH), F32),
            pltpu.VMEM((SUBLANES, LRU_WIDTH), F32),
        ],
        compiler_params=_cparams(("arbitrary", "arbitrary")),
        name="lru",
    )(u2, gg2, conv_w, conv_b, wax, ba, bx, lam)


def _postmix_kernel(x_ref, att_ref, ml_ref, gates_ref, wa_ref, wl_ref, wo_ref, gn_ref,
                    wrT_ref, br_ref, h1_ref, xn_ref, lgT_ref):
    halves = [slice(c * ROW_TILE, (c + 1) * ROW_TILE) for c in range(POSTMIX_TILE // ROW_TILE)]
    y_att = [_dot(att_ref[s, :], wa_ref[...]) for s in halves]
    y_lru = [_dot(ml_ref[s, :], wl_ref[...]) for s in halves]
    merged = []
    for s, ya, yl in zip(halves, y_att, y_lru):
        gates = gates_ref[s, :].astype(F32)
        merged.append((gates[:, :D_MODEL] * ya + gates[:, D_MODEL:] * yl).astype(BF16))
    mixed = [_dot(m, wo_ref[...]) for m in merged]
    for s, mo in zip(halves, mixed):
        h1 = x_ref[s, :] + mo
        h1_ref[s, :] = h1
        xn = _rms(h1, gn_ref[...])
        xn_ref[s, :] = xn
        lgT_ref[:, s] = lax.dot_general(wrT_ref[...], xn, (((1,), (1,)), ((), ())),
                                        precision=lax.Precision.HIGHEST,
                                        preferred_element_type=F32) + br_ref[...]


POSTMIX_TILE = 2 * ROW_TILE


def _postmix(x2, att, ml, gates, wa, wl, wo, gn, wrT, br_col):
    n = x2.shape[0]
    t = POSTMIX_TILE
    const = lambda shape: pl.BlockSpec(shape, lambda i: (0,) * len(shape))
    return pl.pallas_call(
        _postmix_kernel,
        grid=(n // t,),
        in_specs=[
            pl.BlockSpec((t, D_MODEL), lambda i: (i, 0)),
            pl.BlockSpec((t, ATT_WIDTH), lambda i: (i, 0)),
            pl.BlockSpec((t, LRU_WIDTH), lambda i: (i, 0)),
            pl.BlockSpec((t, 2 * D_MODEL), lambda i: (i, 0)),
            const((ATT_WIDTH, D_MODEL)),
            const((LRU_WIDTH, D_MODEL)),
            const((D_MODEL, D_MODEL)),
            const((1, D_MODEL)),
            const((N_EXPERTS, D_MODEL)),
            const((N_EXPERTS, 1)),
        ],
        out_specs=[
            pl.BlockSpec((t, D_MODEL), lambda i: (i, 0)),
            pl.BlockSpec((t, D_MODEL), lambda i: (i, 0)),
            pl.BlockSpec((N_EXPERTS, t), lambda i: (0, i)),
        ],
        out_shape=[
            jax.ShapeDtypeStruct((n, D_MODEL), F32),
            jax.ShapeDtypeStruct((n, D_MODEL), F32),
            jax.ShapeDtypeStruct((N_EXPERTS, n), F32),
        ],
        compiler_params=_cparams(("arbitrary",)),
        name="postmix",
    )(x2, att, ml, gates, wa, wl, wo, gn, wrT, br_col)


def _top4(lg):
    e_iota = lax.broadcasted_iota(jnp.int32, lg.shape, 0).astype(F32)
    picks = []
    for _ in range(TOP_K):
        m = jnp.max(lg, axis=0, keepdims=True)
        first = jnp.min(jnp.where(lg == m, e_iota, float(N_EXPERTS)), axis=0, keepdims=True)
        sel = e_iota == first
        picks.append((sel, m))
        lg = jnp.where(sel, -jnp.inf, lg)
    return picks


def _count_kernel(lgT_ref, cnt_ref, acc):
    i = pl.program_id(0)

    @pl.when(i == 0)
    def _():
        acc[...] = jnp.zeros_like(acc)

    picks = _top4(lgT_ref[...])
    onehot = sum(jnp.where(sel, 1.0, 0.0) for sel, _ in picks)
    part = onehot[:, 0:LANES]
    for c in range(1, ROUTE_TILE // LANES):
        part = part + onehot[:, c * LANES:(c + 1) * LANES]
    acc[...] += part

    @pl.when(i == pl.num_programs(0) - 1)
    def _():
        cnt_ref[...] = jnp.broadcast_to(jnp.sum(acc[...], axis=1, keepdims=True), cnt_ref.shape)


def _count(lgT):
    n = lgT.shape[1]
    return pl.pallas_call(
        _count_kernel,
        grid=(n // ROUTE_TILE,),
        in_specs=[pl.BlockSpec((N_EXPERTS, ROUTE_TILE), lambda i: (0, i))],
        out_specs=pl.BlockSpec((N_EXPERTS, LANES), lambda i: (0, 0)),
        out_shape=jax.ShapeDtypeStruct((N_EXPERTS, LANES), F32),
        scratch_shapes=[pltpu.VMEM((N_EXPERTS, LANES), F32)],
        compiler_params=_cparams(("arbitrary",)),
        name="count",
    )(lgT)


def _route_kernel(lgT_ref, cnt_ref, dest_ref, gcol_ref, bexp_ref, nact_ref, zrow_ref, base, pstart,
                  *, n_blocks_pad):
    i = pl.program_id(0)
    t = ROUTE_TILE

    @pl.when(i == 0)
    def _():
        cnt = cnt_ref[...]
        nblk = jnp.floor((cnt + (MOE_TILE - 1)) * (1.0 / MOE_TILE))
        er = lax.broadcasted_iota(jnp.int32, (N_EXPERTS, N_EXPERTS), 0)
        ec = lax.broadcasted_iota(jnp.int32, (N_EXPERTS, N_EXPERTS), 1)
        lower = jnp.where(ec < er, 1.0, 0.0).astype(BF16)
        blk_start = _dot(lower, nblk.astype(BF16))
        blk_end = blk_start + nblk
        pstart[...] = blk_start * MOE_TILE
        base[...] = jnp.zeros_like(base)
        blk = lax.broadcasted_iota(jnp.int32, (N_EXPERTS, n_blocks_pad), 1).astype(F32)
        ended = jnp.where(blk_end[:, 0:1] <= blk, 1.0, 0.0)
        bexp = jnp.sum(ended, axis=0, keepdims=True)
        bexp_ref[...] = jnp.minimum(bexp, N_EXPERTS - 1).astype(jnp.int32)
        nact_ref[...] = jnp.broadcast_to(blk_end[N_EXPERTS - 1:N_EXPERTS, :], nact_ref.shape).astype(jnp.int32)
        zr = jnp.where(nblk > 0.0, (blk_end - 1.0) * MOE_TILE, -1.0)
        zrow_ref[...] = zr.astype(jnp.int32)

    picks = _top4(lgT_ref[...])
    onehot = sum(jnp.where(sel, 1.0, 0.0) for sel, _ in picks)
    tr = lax.broadcasted_iota(jnp.int32, (t, t), 0)
    tc = lax.broadcasted_iota(jnp.int32, (t, t), 1)
    before = jnp.where(tr < tc, 1.0, 0.0).astype(BF16)
    prefix = _dot(onehot.astype(BF16), before)
    slot = pstart[:, 0:1] + base[:, 0:1] + prefix
    krow = lax.broadcasted_iota(jnp.int32, (SUBLANES, t), 0)
    dest = jnp.zeros((SUBLANES, t), F32)
    for k, (sel, _) in enumerate(picks):
        dest = jnp.where(krow == k, jnp.sum(jnp.where(sel, slot, 0.0), axis=0, keepdims=True), dest)
    dest_ref[...] = dest.astype(jnp.int32)
    part = onehot[:, 0:LANES]
    for c in range(1, t // LANES):
        part = part + onehot[:, c * LANES:(c + 1) * LANES]
    base[...] += jnp.broadcast_to(jnp.sum(part, axis=1, keepdims=True), base.shape)

    vmax = picks[0][1]
    ex = [jnp.exp(v - vmax) for _, v in picks]
    inv = 1.0 / sum(ex)
    grow = lax.broadcasted_iota(jnp.int32, (LANES, t), 0)
    g = jnp.zeros((LANES, t), F32)
    for k, e in enumerate(ex):
        g = jnp.where(grow == k, e * inv, g)
    gcol_ref[...] = g.T


def _route(lgT, cnt, n_blocks_pad):
    n = lgT.shape[1]
    t = ROUTE_TILE
    return pl.pallas_call(
        functools.partial(_route_kernel, n_blocks_pad=n_blocks_pad),
        grid=(n // t,),
        in_specs=[
            pl.BlockSpec((N_EXPERTS, t), lambda i: (0, i)),
            pl.BlockSpec((N_EXPERTS, LANES), lambda i: (0, 0)),
        ],
        out_specs=[
            pl.BlockSpec((SUBLANES, t), lambda i: (0, i)),
            pl.BlockSpec((t, LANES), lambda i: (i, 0)),
            pl.BlockSpec((1, n_blocks_pad), lambda i: (0, 0)),
            pl.BlockSpec((1, LANES), lambda i: (0, 0)),
            pl.BlockSpec((N_EXPERTS, LANES), lambda i: (0, 0)),
        ],
        out_shape=[
            jax.ShapeDtypeStruct((SUBLANES, n), jnp.int32),
            jax.ShapeDtypeStruct((n, LANES), F32),
            jax.ShapeDtypeStruct((1, n_blocks_pad), jnp.int32),
            jax.ShapeDtypeStruct((1, LANES), jnp.int32),
            jax.ShapeDtypeStruct((N_EXPERTS, LANES), jnp.int32),
        ],
        scratch_shapes=[pltpu.VMEM((N_EXPERTS, LANES), F32), pltpu.VMEM((N_EXPERTS, LANES), F32)],
        compiler_params=_cparams(("arbitrary",)),
        name="route",
    )(lgT, cnt)


DISPATCH_TILE = 256


def _dispatch_kernel(nact_ref, zrow_ref, dest_ref, xn_ref, xs_hbm, zbuf, sem, zsem, *, n_blocks):
    i = pl.program_id(0)
    t = DISPATCH_TILE

    def zero_copy(row):
        return pltpu.make_async_copy(zbuf, xs_hbm.at[pl.ds(pl.multiple_of(row, MOE_TILE), MOE_TILE)], zsem)

    @pl.when(i == 0)
    def _():
        zbuf[...] = jnp.zeros_like(zbuf)
        nact = nact_ref[0]
        for phase in ("start", "wait"):
            for e in range(N_EXPERTS):
                row = zrow_ref[e]
                tail = (nact + e) * MOE_TILE

                @pl.when(row >= 0)
                def _():
                    getattr(zero_copy(row), phase)()

                @pl.when(nact + e < n_blocks)
                def _():
                    getattr(zero_copy(tail), phase)()

    def row_copy(group, sub, dst_row):
        return pltpu.make_async_copy(xn_ref.at[group, pl.ds(sub, 1), :], xs_hbm.at[pl.ds(dst_row, 1)], sem)

    def start(g, c):
        for sub in range(SUBLANES):
            for k in range(TOP_K):
                row_copy(g, sub, dest_ref[k, g * SUBLANES + sub]).start(priority=k % 2)
        return c

    def wait(g, c):
        for _ in range(SUBLANES * TOP_K):
            row_copy(0, 0, 0).wait()
        return c

    lax.fori_loop(0, t // SUBLANES, start, 0)
    lax.fori_loop(0, t // SUBLANES, wait, 0)


def _dispatch(nact, zrow, dest, xn, n_blocks):
    n = xn.shape[0]
    t = DISPATCH_TILE
    return pl.pallas_call(
        functools.partial(_dispatch_kernel, n_blocks=n_blocks),
        grid=(n // t,),
        in_specs=[
            pl.BlockSpec(memory_space=pltpu.SMEM),
            pl.BlockSpec(memory_space=pltpu.SMEM),
            pl.BlockSpec((SUBLANES, t), lambda i: (0, i), memory_space=pltpu.SMEM),
            pl.BlockSpec((t // SUBLANES, SUBLANES, D_MODEL), lambda i: (i, 0, 0)),
        ],
        out_specs=pl.BlockSpec(memory_space=pl.ANY),
        out_shape=jax.ShapeDtypeStruct((n_blocks * MOE_TILE, D_MODEL), F32),
        scratch_shapes=[
            pltpu.VMEM((MOE_TILE, D_MODEL), F32),
            pltpu.SemaphoreType.DMA(()),
            pltpu.SemaphoreType.DMA(()),
        ],
        compiler_params=_cparams(("arbitrary",)),
        name="dispatch",
    )(nact, zrow, dest, xn.reshape(n // SUBLANES, SUBLANES, D_MODEL))


CAST_ROWS = 128


def _experts_kernel(bexp_ref, nact_ref, zrow_ref, xs_ref, wgu_hbm, bgu_ref, wdn_hbm, bdn_ref, ys_ref,
                    wgu_f, wdn_f, wgu_b, wdn_b, slot_ref, sems):
    i = pl.program_id(0)
    nact = nact_ref[0]
    active = i < nact
    e = bexp_ref[i]
    fresh = jnp.logical_or(i == 0, e != bexp_ref[jnp.maximum(i - 1, 0)])

    def weight_copies(expert, s):
        return (pltpu.make_async_copy(wgu_hbm.at[expert], wgu_f.at[s], sems.at[0, s]),
                pltpu.make_async_copy(wdn_hbm.at[expert], wdn_f.at[s], sems.at[1, s]))

    @pl.when(i == 0)
    def _():
        slot_ref[0] = 1
        for cp in weight_copies(e, 0):
            cp.start()

    @pl.when(jnp.logical_and(active, fresh))
    def _():
        s = 1 - slot_ref[0]
        slot_ref[0] = s
        for cp in weight_copies(e, s):
            cp.wait()
        next_blk = zrow_ref[e] // MOE_TILE + 1

        @pl.when(next_blk < nact)
        def _():
            for cp in weight_copies(bexp_ref[next_blk], 1 - s):
                cp.start()

        def cast_gu(r, c):
            rows = pl.ds(pl.multiple_of(r * CAST_ROWS, CAST_ROWS), CAST_ROWS)
            wgu_b[rows, :] = wgu_f[s, rows, :].astype(BF16)
            return c

        def cast_dn(r, c):
            rows = pl.ds(pl.multiple_of(r * CAST_ROWS, CAST_ROWS), CAST_ROWS)
            wdn_b[rows, :] = wdn_f[s, rows, :].astype(BF16)
            return c

        lax.fori_loop(0, D_MODEL // CAST_ROWS, cast_gu, 0)
        lax.fori_loop(0, D_EXPERT // CAST_ROWS, cast_dn, 0)

    @pl.when(active)
    def _():
        x = xs_ref[...].astype(BF16)
        hgu = _dot(x, wgu_b[...]) + bgu_ref[0]
        gate = jnp.minimum(hgu[:, :D_EXPERT], SWIGLU_LIMIT)
        up = jnp.clip(hgu[:, D_EXPERT:], -SWIGLU_LIMIT, SWIGLU_LIMIT)
        glu = gate * _sigmoid(gate * SWIGLU_ALPHA)
        act = ((up + 1.0) * glu).astype(BF16)
        ys_ref[...] = _dot(act, wdn_b[...]) + bdn_ref[0]

    @pl.when(i >= nact_ref[0])
    def _():
        ys_ref[...] = jnp.zeros_like(ys_ref)


def _experts(bexp, nact, zrow, xs, wgu, bgu, wdn, bdn, n_blocks):
    def blk(i, be, na, zr):
        return jnp.minimum(i, na[0] - 1)

    grid_spec = pltpu.PrefetchScalarGridSpec(
        num_scalar_prefetch=3,
        grid=(n_blocks,),
        in_specs=[
            pl.BlockSpec((MOE_TILE, D_MODEL), lambda i, be, na, zr: (blk(i, be, na, zr), 0)),
            pl.BlockSpec(memory_space=pl.ANY),
            pl.BlockSpec((1, 1, 2 * D_EXPERT), lambda i, be, na, zr: (be[blk(i, be, na, zr)], 0, 0)),
            pl.BlockSpec(memory_space=pl.ANY),
            pl.BlockSpec((1, 1, D_MODEL), lambda i, be, na, zr: (be[blk(i, be, na, zr)], 0, 0)),
        ],
        out_specs=pl.BlockSpec((MOE_TILE, D_MODEL), lambda i, be, na, zr: (i, 0)),
        scratch_shapes=[
            pltpu.VMEM((2, D_MODEL, 2 * D_EXPERT), F32),
            pltpu.VMEM((2, D_EXPERT, D_MODEL), F32),
            pltpu.VMEM((D_MODEL, 2 * D_EXPERT), BF16),
            pltpu.VMEM((D_EXPERT, D_MODEL), BF16),
            pltpu.SMEM((1,), jnp.int32),
            pltpu.SemaphoreType.DMA((2, 2)),
        ],
    )
    return pl.pallas_call(
        _experts_kernel,
        grid_spec=grid_spec,
        out_shape=jax.ShapeDtypeStruct((n_blocks * MOE_TILE, D_MODEL), F32),
        compiler_params=_cparams(("arbitrary",)),
        name="experts",
    )(bexp, nact, zrow, xs, wgu, bgu, wdn, bdn)


FINAL_TILE = 256


def _final_kernel(dest_ref, dnext_ref, ys_hbm, h1_ref, gcol_ref, p_ref, gpl_ref, wpg_ref, bpg_ref,
                  wpp_ref, gpost_ref, gfin_ref, o_ref, rows, sems):
    i = pl.program_id(0)
    t = FINAL_TILE
    slot = lax.rem(i, 2)

    def row_copy(src_row, s, k, group, sub):
        return pltpu.make_async_copy(ys_hbm.at[pl.ds(src_row, 1)],
                                     rows.at[s, k, group, pl.ds(sub, 1), :], sems.at[s])

    def gather(d_ref, s):
        def start(g, c):
            for sub in range(SUBLANES):
                for k in range(TOP_K):
                    row_copy(d_ref[k, g * SUBLANES + sub], s, k, g, sub).start(priority=k % 2)
            return c
        lax.fori_loop(0, t // SUBLANES, start, 0)

    @pl.when(i == 0)
    def _():
        gather(dest_ref, 0)

    @pl.when(i + 1 < pl.num_programs(0))
    def _():
        gather(dnext_ref, 1 - slot)

    def wait(g, c):
        for _ in range(SUBLANES * TOP_K):
            row_copy(0, slot, 0, 0, 0).wait()
        return c

    lax.fori_loop(0, t // SUBLANES, wait, 0)

    g = gcol_ref[...]
    moe = g[:, 0:1] * rows[slot, 0].reshape(t, D_MODEL)
    for k in range(1, TOP_K):
        moe = moe + g[:, k:k + 1] * rows[slot, k].reshape(t, D_MODEL)
    h2 = h1_ref[...] + moe
    hn = _rms(h2, gpl_ref[...]).astype(BF16)
    pl_gate = _sigmoid(_dot(hn, wpg_ref[...]) + bpg_ref[...])
    pl_in = _rms(_dot(p_ref[...].astype(BF16), wpp_ref[...]), gpost_ref[...])
    h3 = h2 + pl_gate * pl_in
    o_ref[...] = _rms(h3, gfin_ref[...])


def _final(dest, ys, h1, gcol, p2, gpl, wpg, bpg, wpp, gpost, gfin):
    n = h1.shape[0]
    t = FINAL_TILE
    last = n // t - 1
    const = lambda shape: pl.BlockSpec(shape, lambda i: (0,) * len(shape))
    return pl.pallas_call(
        _final_kernel,
        grid=(n // t,),
        in_specs=[
            pl.BlockSpec((SUBLANES, t), lambda i: (0, i), memory_space=pltpu.SMEM),
            pl.BlockSpec((SUBLANES, t), lambda i: (0, jnp.minimum(i + 1, last)), memory_space=pltpu.SMEM),
            pl.BlockSpec(memory_space=pl.ANY),
            pl.BlockSpec((t, D_MODEL), lambda i: (i, 0)),
            pl.BlockSpec((t, LANES), lambda i: (i, 0)),
            pl.BlockSpec((t, PL_DIM), lambda i: (i, 0)),
            const((1, D_MODEL)),
            const((D_MODEL, D_MODEL)),
            const((1, D_MODEL)),
            const((PL_DIM, D_MODEL)),
            const((1, D_MODEL)),
            const((1, D_MODEL)),
        ],
        out_specs=pl.BlockSpec((t, D_MODEL), lambda i: (i, 0)),
        out_shape=jax.ShapeDtypeStruct((n, D_MODEL), F32),
        scratch_shapes=[pltpu.VMEM((2, TOP_K, t // SUBLANES, SUBLANES, D_MODEL), F32),
                        pltpu.SemaphoreType.DMA((2,))],
        compiler_params=_cparams(("arbitrary",)),
        name="final",
    )(dest, dest, ys, h1, gcol, p2, gpl, wpg, bpg, wpp, gpost, gfin)


def _layer(h, p_i, norm_mix_g, w_in, conv_w, conv_b, lru_wa, lru_ba, lru_wx, lru_bx, lru_lambda,
           w_att_out, w_lru_out, w_merge, b_merge, w_out, norm_moe_g, w_router, b_router,
           w_gate_up, b_gate_up, w_down, b_down, norm_pl_g, w_pl_gate, b_pl_gate, w_pl_proj,
           norm_pl_post_g, norm_final_g):
    batch, seq, _ = h.shape
    n = batch * seq
    x2 = h.reshape(n, D_MODEL)
    row = lambda v: v.reshape(1, -1)
    a = ATT_WIDTH
    wq, wk, wv = w_in[:, :a], w_in[:, a:2 * a], w_in[:, 2 * a:3 * a]
    wu, wg = w_in[:, 3 * a:3 * a + LRU_WIDTH], w_in[:, 3 * a + LRU_WIDTH:]
    qT3, k2, vT3, u2, gg2, gates = _proj(
        x2, row(norm_mix_g), wq.T.astype(BF16), wk.astype(BF16), wv.T.astype(BF16),
        wu.astype(BF16), wg.astype(BF16), w_merge.astype(BF16), row(b_merge))
    att = _attention(qT3, k2, vT3, batch, seq)
    wax = jnp.concatenate([lru_wa, lru_wx], axis=-1).astype(BF16)
    ml = _lru(u2, gg2, conv_w, row(conv_b), wax, row(lru_ba), row(lru_bx), row(lru_lambda), batch, seq)
    h1, xn2, lgT = _postmix(x2, att, ml, gates, w_att_out.astype(BF16), w_lru_out.astype(BF16),
                            w_out.astype(BF16), row(norm_moe_g), w_router.T, b_router.reshape(-1, 1))
    n_blocks = (n * TOP_K) // MOE_TILE + N_EXPERTS
    n_blocks_pad = -(-n_blocks // LANES) * LANES
    cnt = _count(lgT)
    dest, gcol, bexp, nact, zrow = _route(lgT, cnt, n_blocks_pad)
    nact1 = nact[0, :1]
    zrow1 = zrow[:, 0]
    xs = _dispatch(nact1, zrow1, dest, xn2, n_blocks)
    ys = _experts(bexp[0], nact1, zrow1, xs, w_gate_up, b_gate_up[:, None, :], w_down,
                  b_down[:, None, :], n_blocks)
    out = _final(dest, ys, h1, gcol, p_i.reshape(n, PL_DIM), row(norm_pl_g), w_pl_gate.astype(BF16),
                 row(b_pl_gate), w_pl_proj.astype(BF16), row(norm_pl_post_g), row(norm_final_g))
    return out.reshape(batch, seq, D_MODEL)


def kernel(x, p, norm_mix_g, w_in, conv_w, conv_b, lru_wa, lru_ba, lru_wx, lru_bx, lru_lambda, w_att_out, w_lru_out, w_merge, b_merge, w_out, norm_moe_g, w_router, b_router, w_gate_up, b_gate_up, w_down, b_down, norm_pl_g, w_pl_gate, b_pl_gate, w_pl_proj, norm_pl_post_g, norm_final_g):
    depth = p.shape[0]
    assert depth == 1, "single-layer trunk: the final norm is fused into the layer's last kernel"
    i = 0
    return _layer(x, p[i], norm_mix_g[i], w_in[i], conv_w[i], conv_b[i], lru_wa[i], lru_ba[i],
                  lru_wx[i], lru_bx[i], lru_lambda[i], w_att_out[i], w_lru_out[i], w_merge[i],
                  b_merge[i], w_out[i], norm_moe_g[i], w_router[i], b_router[i], w_gate_up[i],
                  b_gate_up[i], w_down[i], b_down[i], norm_pl_g[i], w_pl_gate[i], b_pl_gate[i],
                  w_pl_proj[i], norm_pl_post_g[i], norm_final_g)
```
